```python
import jax, jax.numpy as jnp
from jax import lax
import numpy as np

D_MODEL = 1024
BATCH = 4
SEQ = 4096
DEPTH = 1
DEC_BATCH = 32
DEC_SEQ = 4
PAST_LEN = 16384
PAGE_SIZE = 128

N_HEADS = 8
HEAD_DIM = 64
N_KV_HEADS = 2
N_IDX_HEADS = 4
IDX_DIM = 64
CONV_DIM = 512
CONV_WIDTH = 3
D_FF = 2816
N_META = 16
TOPK_MAX = 256
TOPK_FRACTION = 4
Q_BLOCK = 128
ROPE_THETA = 500000.0
ROPE_FRACTION = 4
NORM_EPS = 1e-6

kernel_name = "dsa_shortconv_gated_hybrid_step"


def _split_points():
    sizes = [N_HEADS * HEAD_DIM, N_KV_HEADS * HEAD_DIM, N_KV_HEADS * HEAD_DIM,
             N_IDX_HEADS * IDX_DIM, IDX_DIM, N_IDX_HEADS,
             CONV_DIM, CONV_DIM, CONV_DIM, D_MODEL, D_MODEL]
    pts, acc = [], 0
    for s in sizes[:-1]:
        acc += s
        pts.append(acc)
    return pts


def _proj_width():
    return (N_HEADS * HEAD_DIM + 2 * N_KV_HEADS * HEAD_DIM + N_IDX_HEADS * IDX_DIM + IDX_DIM
            + N_IDX_HEADS + 3 * CONV_DIM + 2 * D_MODEL)


def rms_norm(x, g):
    x32 = x.astype(jnp.float32)
    r = lax.rsqrt(jnp.mean(x32 * x32, axis=-1, keepdims=True) + NORM_EPS)
    return (x32 * r).astype(x.dtype) * g


def partial_rope(x, pos):
    rot = x.shape[-1] // ROPE_FRACTION
    half = rot // 2
    inv = jnp.power(ROPE_THETA, -jnp.arange(half, dtype=jnp.float32) * 2.0 / rot)
    ang = pos.astype(jnp.float32)[:, None] * inv[None, :]
    cos = jnp.cos(ang)[:, None, :].astype(x.dtype)
    sin = jnp.sin(ang)[:, None, :].astype(x.dtype)
    x1, x2, xp = x[..., :half], x[..., half:rot], x[..., rot:]
    return jnp.concatenate([x1 * cos - x2 * sin, x2 * cos + x1 * sin, xp], axis=-1)


def causal_dwconv(u, prev, w):
    T = u.shape[1]
    full = jnp.concatenate([prev.astype(u.dtype), u], axis=1)
    out = full[:, 0:T] * w[0]
    for j in range(1, CONV_WIDTH):
        out = out + full[:, j:j + T] * w[j]
    return out, full[:, T:]


def indexer_scores(qi, wi, ki):
    s = jax.nn.relu(jnp.einsum('bqhd,bsd->bqhs', qi, ki))
    return jnp.einsum('bqhs,bqh->bqs', s, wi).astype(jnp.float32)


def sparse_attend(q, k_sel, v_sel, valid):
    Bn, Q = q.shape[0], q.shape[1]
    qg = q.reshape(Bn, Q, N_KV_HEADS, N_HEADS // N_KV_HEADS, HEAD_DIM)
    logits = jnp.einsum('bqngd,bqsnd->bqngs', qg, k_sel).astype(jnp.float32) * (HEAD_DIM ** -0.5)
    logits = jnp.where(valid[:, :, None, None, :], logits, -jnp.inf)
    p = jax.nn.softmax(logits, axis=-1).astype(v_sel.dtype)
    o = jnp.einsum('bqngs,bqsnd->bqngd', p, v_sel)
    return o.reshape(Bn, Q, N_HEADS * HEAD_DIM)


def make_prompt_attend(T, k_top):
    def attend(q, k, v, qi, ki, wi):
        Bn = q.shape[0]
        n_blk = -(-T // Q_BLOCK)
        pad = n_blk * Q_BLOCK - T

        def blocks(a):
            a = jnp.pad(a, [(0, 0), (0, pad)] + [(0, 0)] * (a.ndim - 2))
            return jnp.moveaxis(a.reshape((Bn, n_blk, Q_BLOCK) + a.shape[2:]), 1, 0)

        qpos = jnp.arange(n_blk * Q_BLOCK).reshape(n_blk, Q_BLOCK)
        kpos = jnp.arange(T)
        bidx = jnp.arange(Bn)[:, None, None]

        def one_block(xs):
            qb, qib, wib, qp = xs
            scores = indexer_scores(qib, wib, ki)
            scores = jnp.where(kpos[None, None, :] <= qp[None, :, None], scores, -jnp.inf)
            vals, idx = lax.top_k(scores, k_top)
            return sparse_attend(qb, k[bidx, idx], v[bidx, idx], jnp.isfinite(vals))

        out = lax.map(one_block, (blocks(q), blocks(qi), blocks(wi), qpos))
        return jnp.moveaxis(out, 0, 1).reshape(Bn, n_blk * Q_BLOCK, -1)[:, :T]
    return attend


def make_sample_attend(cache_k_l, cache_v_l, cache_kidx_l, page_table, k_top):
    def attend(q, k, v, qi, ki, wi):
        Bn, DS = q.shape[0], q.shape[1]
        L = PAST_LEN + DS
        ki_past = cache_kidx_l[page_table].reshape(Bn, PAST_LEN, IDX_DIM).astype(ki.dtype)
        ki_all = jnp.concatenate([ki_past, ki], axis=1)
        scores = indexer_scores(qi, wi, ki_all)
        qpos = PAST_LEN + jnp.arange(DS)
        kpos = jnp.arange(L)
        scores = jnp.where(kpos[None, None, :] <= qpos[None, :, None], scores, -jnp.inf)
        vals, idx = lax.top_k(scores, k_top)
        bidx = jnp.arange(Bn)[:, None, None]
        past_i = jnp.minimum(idx, PAST_LEN - 1)
        phys = page_table[bidx, past_i // PAGE_SIZE]
        off = past_i % PAGE_SIZE
        new_i = jnp.clip(idx - PAST_LEN, 0, DS - 1)
        is_new = (idx >= PAST_LEN)[..., None, None]
        k_sel = jnp.where(is_new, k[bidx, new_i], cache_k_l[phys, off].astype(k.dtype))
        v_sel = jnp.where(is_new, v[bidx, new_i], cache_v_l[phys, off].astype(v.dtype))
        return sparse_attend(q, k_sel, v_sel, jnp.isfinite(vals))
    return attend


def trunk_layer(x, pos, attend, conv_prev, ffn_prev, w_in, w_pa, w_pb, w_o, conv_w,
                norm1, norm2, ffn_up, ffn_conv_w, ffn_down):
    Bn, T, _ = x.shape
    hn = rms_norm(x, norm1)
    proj = hn @ w_in
    q, k, v, qi, ki, wi, cb, cc, cx, ga, gb = jnp.split(proj, _split_points(), axis=-1)
    q = partial_rope(q.reshape(Bn, T, N_HEADS, HEAD_DIM), pos)
    k = partial_rope(k.reshape(Bn, T, N_KV_HEADS, HEAD_DIM), pos)
    v = v.reshape(Bn, T, N_KV_HEADS, HEAD_DIM)
    qi = partial_rope(qi.reshape(Bn, T, N_IDX_HEADS, IDX_DIM), pos)
    ki = partial_rope(ki[:, :, None, :], pos)[:, :, 0, :]
    wi = wi * ((N_IDX_HEADS * IDX_DIM) ** -0.5)
    a = attend(q, k, v, qi, ki, wi)
    u, conv_new = causal_dwconv(cc * cx, conv_prev, conv_w)
    b = cb * u
    mixed = jax.nn.sigmoid(ga) * (a @ w_pa) + jax.nn.sigmoid(gb) * (b @ w_pb)
    x = x + mixed @ w_o
    hn2 = rms_norm(x, norm2)
    ua, ub = jnp.split(hn2 @ ffn_up, 2, axis=-1)
    ua_c, ffn_new = causal_dwconv(ua, ffn_prev, ffn_conv_w)
    x = x + (jax.nn.silu(ua_c) * ub) @ ffn_down
    return x, k, v, ki, conv_new, ffn_new


def setup_inputs(seed: int = 0) -> dict:
    key = jax.random.key(seed)
    ks = jax.random.split(key, 24)
    n_pages = PAST_LEN // PAGE_SIZE
    n_used = DEC_BATCH * n_pages
    n_pool = n_used + n_used // 4
    f32 = jnp.float32
    nrm = lambda k, shp, s: jax.random.normal(k, shp, f32) * s
    page_table = jax.random.permutation(ks[0], n_pool)[:n_used].reshape(DEC_BATCH, n_pages).astype(jnp.int32)
    return {
        "x_prompt": nrm(ks[1], (BATCH, SEQ, D_MODEL), 1.0),
        "x_sample": nrm(ks[2], (DEC_BATCH, DEC_SEQ, D_MODEL), 1.0),
        "cache_k": nrm(ks[3], (DEPTH, n_pool, PAGE_SIZE, N_KV_HEADS, HEAD_DIM), 1.0),
        "cache_v": nrm(ks[4], (DEPTH, n_pool, PAGE_SIZE, N_KV_HEADS, HEAD_DIM), 1.0),
        "cache_kidx": nrm(ks[5], (DEPTH, n_pool, PAGE_SIZE, IDX_DIM), 1.0),
        "state_conv": nrm(ks[6], (DEPTH, DEC_BATCH, CONV_WIDTH - 1, CONV_DIM), 1.0),
        "state_ffn_conv": nrm(ks[7], (DEPTH, DEC_BATCH, CONV_WIDTH - 1, D_FF), 1.0),
        "page_table": page_table,
        "meta_tokens": nrm(ks[8], (N_META, D_MODEL), 1.0),
        "w_in": nrm(ks[9], (DEPTH, D_MODEL, _proj_width()), D_MODEL ** -0.5),
        "w_pa": nrm(ks[10], (DEPTH, N_HEADS * HEAD_DIM, D_MODEL), (N_HEADS * HEAD_DIM) ** -0.5),
        "w_pb": nrm(ks[11], (DEPTH, CONV_DIM, D_MODEL), CONV_DIM ** -0.5),
        "w_o": nrm(ks[12], (DEPTH, D_MODEL, D_MODEL), D_MODEL ** -0.5),
        "conv_w": nrm(ks[13], (DEPTH, CONV_WIDTH, CONV_DIM), CONV_WIDTH ** -0.5),
        "norm1": 1.0 + nrm(ks[14], (DEPTH, D_MODEL), 0.01),
        "norm2": 1.0 + nrm(ks[15], (DEPTH, D_MODEL), 0.01),
        "ffn_up": nrm(ks[16], (DEPTH, D_MODEL, 2 * D_FF), D_MODEL ** -0.5),
        "ffn_conv_w": nrm(ks[17], (DEPTH, CONV_WIDTH, D_FF), CONV_WIDTH ** -0.5),
        "ffn_down": nrm(ks[18], (DEPTH, D_FF, D_MODEL), D_FF ** -0.5),
        "norm_final": 1.0 + nrm(ks[19], (D_MODEL,), 0.01),
    }


def reference(x_prompt, x_sample, cache_k, cache_v, cache_kidx, state_conv, state_ffn_conv, page_table,
              meta_tokens, w_in, w_pa, w_pb, w_o, conv_w, norm1, norm2, ffn_up, ffn_conv_w, ffn_down,
              norm_final):
    Bp = x_prompt.shape[0]
    T_p = SEQ + N_META
    k_top_p = min(TOPK_MAX, T_p // TOPK_FRACTION)
    k_top_s = min(TOPK_MAX, (PAST_LEN + DEC_SEQ) // TOPK_FRACTION)
    pos_p = jnp.arange(T_p)
    pos_s = PAST_LEN + jnp.arange(DEC_SEQ)

    meta = jnp.broadcast_to(meta_tokens[None].astype(x_prompt.dtype), (Bp, N_META, D_MODEL))
    h_p = jnp.concatenate([meta, x_prompt], axis=1)
    h_s = x_sample
    prompt_attend = make_prompt_attend(T_p, k_top_p)

    kp, vp, kip, cp, fp = [], [], [], [], []
    ksm, vsm, kism, csm, fsm = [], [], [], [], []
    for l in range(DEPTH):
        weights = (w_in[l], w_pa[l], w_pb[l], w_o[l], conv_w[l], norm1[l], norm2[l],
                   ffn_up[l], ffn_conv_w[l], ffn_down[l])
        zc = jnp.zeros((Bp, CONV_WIDTH - 1, CONV_DIM), h_p.dtype)
        zf = jnp.zeros((Bp, CONV_WIDTH - 1, D_FF), h_p.dtype)
        h_p, k1, v1, ki1, c1, f1 = trunk_layer(h_p, pos_p, prompt_attend, zc, zf, *weights)
        sample_attend = make_sample_attend(cache_k[l], cache_v[l], cache_kidx[l], page_table, k_top_s)
        h_s, k2, v2, ki2, c2, f2 = trunk_layer(h_s, pos_s, sample_attend, state_conv[l],
                                               state_ffn_conv[l], *weights)
        kp.append(k1); vp.append(v1); kip.append(ki1); cp.append(c1); fp.append(f1)
        ksm.append(k2); vsm.append(v2); kism.append(ki2); csm.append(c2); fsm.append(f2)

    y_prompt = rms_norm(h_p, norm_final)[:, N_META:]
    y_sample = rms_norm(h_s, norm_final)
    return (y_prompt, y_sample,
            jnp.stack(kp), jnp.stack(vp), jnp.stack(kip), jnp.stack(cp), jnp.stack(fp),
            jnp.stack(ksm), jnp.stack(vsm), jnp.stack(kism), jnp.stack(csm), jnp.stack(fsm))
```

```python
import functools

import jax
import jax.numpy as jnp
from jax import lax
from jax.experimental import pallas as pl
from jax.experimental.pallas import tpu as pltpu

D_MODEL = 1024
N_HEADS = 8
HEAD_DIM = 64
N_KV_HEADS = 2
N_IDX_HEADS = 4
IDX_DIM = 64
CONV_DIM = 512
CONV_WIDTH = 3
D_FF = 2816
N_META = 16
TOPK_MAX = 256
TOPK_FRACTION = 4
ROPE_THETA = 500000.0
ROPE_FRACTION = 4
NORM_EPS = 1e-6
PAGE_SIZE = 128

LANE = 128
SUBLANE = 8
ROW_TILE = 384
FF_CHUNK = 256
VMEM_LIMIT = 56 * 1024 * 1024

NEG = -1e30
INT_MIN = -2 ** 31
BIG_POS = 1 << 24

F32 = jnp.float32
BF16 = jnp.bfloat16
I32 = jnp.int32

_NT = (((1,), (1,)), ((), ()))


def _cparams(*sem):
    return pltpu.CompilerParams(dimension_semantics=sem, vmem_limit_bytes=VMEM_LIMIT)


def _const_spec(shape):
    nd = len(shape)
    return pl.BlockSpec(shape, lambda *_: (0,) * nd, pipeline_mode=pl.Buffered(1))


def _proj_kernel(x_ref, g_ref, watt_ref, wc_ref, wg_ref, c_ref, s1_ref, s2_ref,
                 q_ref, k_ref, v_ref, kpad_ref, qi_ref, ki_ref, kib_ref, wi_ref,
                 cb_ref, ccx_ref, sga_ref, sgb_ref):
    x = x_ref[...]
    r = lax.rsqrt(jnp.mean(x * x, axis=-1, keepdims=True) + NORM_EPS)
    hn = ((x * r) * g_ref[...]).astype(BF16)

    cos_t = c_ref[...]
    sin_a = s1_ref[...]
    sin_b = s2_ref[...]

    def rope(t):
        return t * cos_t + pltpu.roll(t, 8, 1) * sin_a + pltpu.roll(t, LANE - 8, 1) * sin_b

    att = jnp.dot(hn, watt_ref[...], preferred_element_type=F32)
    for j in range(4):
        q_ref[:, j * LANE:(j + 1) * LANE] = (rope(att[:, j * LANE:(j + 1) * LANE]) * (HEAD_DIM ** -0.5)).astype(BF16)
    k = rope(att[:, 512:640])
    k_ref[...] = k
    ksw = pltpu.roll(k, 64, 1)
    low = lax.broadcasted_iota(I32, k.shape, 1) < 64
    zero = jnp.zeros_like(k)
    kpad_ref[:, 0:128] = jnp.where(low, k, zero).astype(BF16)
    kpad_ref[:, 128:256] = jnp.where(low, zero, ksw).astype(BF16)
    kpad_ref[:, 256:384] = jnp.where(low, ksw, zero).astype(BF16)
    kpad_ref[:, 384:512] = jnp.where(low, zero, k).astype(BF16)
    v_ref[...] = att[:, 640:768]
    for j in range(2):
        qi_ref[:, j * LANE:(j + 1) * LANE] = rope(att[:, 768 + j * LANE:768 + (j + 1) * LANE]).astype(BF16)
    kit = rope(att[:, 1024:1152])
    ki_ref[...] = kit[:, :IDX_DIM]
    kib_ref[:, 0:128] = kit.astype(BF16)
    kib_ref[:, 128:256] = pltpu.roll(kit, 64, 1).astype(BF16)
    wi_ref[...] = att[:, 1152:1280] * ((N_IDX_HEADS * IDX_DIM) ** -0.5)

    c3 = jnp.dot(hn, wc_ref[...], preferred_element_type=F32)
    cb_ref[...] = c3[:, :CONV_DIM]
    ccx_ref[...] = c3[:, CONV_DIM:2 * CONV_DIM] * c3[:, 2 * CONV_DIM:]
    g2 = jnp.dot(hn, wg_ref[...], preferred_element_type=F32)
    sga_ref[...] = jax.nn.sigmoid(g2[:, :D_MODEL])
    sgb_ref[...] = jax.nn.sigmoid(g2[:, D_MODEL:])


def _proj(x, g, watt, wc, wg, cos_t, sin_a, sin_b, tm):
    n = x.shape[0]
    tab_tiles = cos_t.shape[0] // tm
    row = lambda w: pl.BlockSpec((tm, w), lambda i: (i, 0))
    tab = pl.BlockSpec((tm, LANE), lambda i: (i % tab_tiles, 0))
    outs = [(512, BF16), (128, F32), (128, F32), (512, BF16), (256, BF16), (IDX_DIM, F32), (256, BF16),
            (128, F32), (CONV_DIM, F32), (CONV_DIM, F32), (D_MODEL, F32), (D_MODEL, F32)]
    return pl.pallas_call(
        _proj_kernel,
        grid=(n // tm,),
        in_specs=[row(D_MODEL), _const_spec(g.shape), _const_spec(watt.shape), _const_spec(wc.shape),
                  _const_spec(wg.shape), tab, tab, tab],
        out_specs=[row(w) for w, _ in outs],
        out_shape=[jax.ShapeDtypeStruct((n, w), dt) for w, dt in outs],
        compiler_params=_cparams("arbitrary"),
        name="proj",
    )(x, g, watt, wc, wg, cos_t, sin_a, sin_b)


def _order_key(s):
    s = jnp.where(s == 0.0, 0.0, s)
    b = pltpu.bitcast(s, I32)
    return b ^ ((b >> 31) & 0x7FFFFFFF)


def _select_threshold(kt_ref, npairs, k_top, pos_bits):
    row = lax.broadcasted_iota(I32, (LANE, LANE), 0)

    def count(pred):
        def body(pi, acc):
            for u in range(2):
                c = 2 * pi + u
                hit = pred(kt_ref[c], c).astype(I32)
                acc = acc + hit.reshape(LANE // SUBLANE, SUBLANE, LANE).sum(axis=0)
            return acc
        acc = lax.fori_loop(0, npairs, body, jnp.zeros((SUBLANE, LANE), I32))
        return acc.sum(axis=0, keepdims=True)

    def value_step(b, thr):
        cand = thr + lax.shift_left(jnp.int32(1), 31 - b)
        cnt = count(lambda blk, c: blk >= cand)
        return jnp.where(cnt >= k_top, cand, thr)

    thr = lax.fori_loop(0, 32, value_step, jnp.full((1, LANE), INT_MIN, I32))
    thr = jnp.maximum(thr, INT_MIN + 1)
    n_ge = count(lambda blk, c: blk >= thr)
    n_gt = count(lambda blk, c: blk > thr)
    tie = n_ge > k_top
    need = k_top - n_gt
    cut0 = jnp.where(tie, 0, BIG_POS)

    def tie_search():
        def pos_step(b, cut):
            cand = cut + lax.shift_left(jnp.int32(1), pos_bits - 1 - b)
            below = count(lambda blk, c: (blk == thr) & ((c * LANE + row) < cand))
            return jnp.where(tie & (below < need), cand, cut)
        return lax.fori_loop(0, pos_bits, pos_step, cut0)

    cut = lax.cond(jnp.max(tie.astype(I32)) > 0, tie_search, lambda: cut0)
    return thr, cut


def _attn_kernel(q_ref, qi_ref, wi_ref, kpad_ref, kib_ref, v_ref, o_ref,
                 kt_ref, vte_ref, acc_ref, m_ref, *, k_top, pos_bits, n_chunks):
    i = pl.program_id(1)
    row = lax.broadcasted_iota(I32, (LANE, LANE), 0)
    col = lax.broadcasted_iota(I32, (LANE, LANE), 1)

    @pl.when(i == 0)
    def _():
        def tb(c, carry):
            vt = v_ref[0, pl.ds(pl.multiple_of(c * LANE, LANE), LANE), :].T
            for n in range(N_KV_HEADS):
                vte_ref[n, c, 0:HEAD_DIM, :] = vt[n * HEAD_DIM:(n + 1) * HEAD_DIM].astype(BF16)
                vte_ref[n, c, HEAD_DIM:, :] = jnp.ones((16, LANE), BF16)
            return carry
        lax.fori_loop(0, n_chunks, tb, 0)

    qis = jnp.concatenate([qi_ref[0, :, 0:LANE], qi_ref[0, :, LANE:2 * LANE]], axis=0)
    wit = wi_ref[0].T
    w = [wit[h:h + 1, :] for h in range(N_IDX_HEADS)]

    def score_chunk(c, carry):
        kk = kib_ref[0, pl.ds(pl.multiple_of(c * LANE, LANE), LANE), :]
        se = lax.dot_general(kk[:, :LANE], qis, _NT, preferred_element_type=F32)
        so = lax.dot_general(kk[:, LANE:], qis, _NT, preferred_element_type=F32)
        s = (jnp.maximum(se[:, :LANE], 0.0) * w[0] + jnp.maximum(so[:, :LANE], 0.0) * w[1]
             + jnp.maximum(se[:, LANE:], 0.0) * w[2] + jnp.maximum(so[:, LANE:], 0.0) * w[3])
        allowed = (c * LANE + row) <= (i * LANE + col)
        kt_ref[c] = jnp.where(allowed, _order_key(s), INT_MIN)
        return carry

    lax.fori_loop(0, i + 1, score_chunk, 0)
    kt_ref[i + 1] = jnp.full((LANE, LANE), INT_MIN, I32)

    thr, cut = _select_threshold(kt_ref, lax.shift_right_logical(i + 2, 1), k_top, pos_bits)

    m_ref[...] = jnp.full(m_ref.shape, NEG, F32)
    acc_ref[...] = jnp.zeros(acc_ref.shape, F32)
    qs = [jnp.concatenate([q_ref[0, :, (2 * n) * LANE:(2 * n + 1) * LANE],
                           q_ref[0, :, (2 * n + 1) * LANE:(2 * n + 2) * LANE]], axis=0)
          for n in range(N_KV_HEADS)]

    def attend_chunk(c, carry):
        kt = kt_ref[c]
        sel = (kt > thr) | ((kt == thr) & ((c * LANE + row) <= cut))
        bias = jnp.where(sel, 0.0, NEG)
        bias4 = jnp.concatenate([bias] * 4, axis=1)
        kp = kpad_ref[0, pl.ds(pl.multiple_of(c * LANE, LANE), LANE), :]
        for n in range(N_KV_HEADS):
            le = lax.dot_general(kp[:, (2 * n) * LANE:(2 * n + 1) * LANE], qs[n], _NT,
                                 preferred_element_type=F32)
            lo = lax.dot_general(kp[:, (2 * n + 1) * LANE:(2 * n + 2) * LANE], qs[n], _NT,
                                 preferred_element_type=F32)
            l4 = jnp.concatenate([le, lo], axis=1) + bias4
            m_old = m_ref[n]
            m_new = jnp.maximum(m_old, jnp.max(l4, axis=0, keepdims=True))
            p = jnp.exp(l4 - m_new).astype(BF16)
            pv = jnp.dot(vte_ref[n, c], p, preferred_element_type=F32)
            acc_ref[n] = acc_ref[n] * jnp.exp(m_old - m_new) + pv
            m_ref[n] = m_new
        return carry

    lax.fori_loop(0, i + 1, attend_chunk, 0)

    for n in range(N_KV_HEADS):
        acc = acc_ref[n]
        o = acc[0:HEAD_DIM] / acc[HEAD_DIM:HEAD_DIM + 1]
        t0 = jnp.concatenate([o[:, 0:LANE], o[:, 2 * LANE:3 * LANE]], axis=0)
        t1 = jnp.concatenate([o[:, LANE:2 * LANE], o[:, 3 * LANE:4 * LANE]], axis=0)
        o_ref[0, :, (2 * n) * LANE:(2 * n + 1) * LANE] = t0.T.astype(BF16)
        o_ref[0, :, (2 * n + 1) * LANE:(2 * n + 2) * LANE] = t1.T.astype(BF16)


def _prompt_attend(q, qi, wi, kpad, kib, v, k_top):
    b, tp, _ = q.shape
    n_chunks = tp // LANE
    pos_bits = max(1, (tp + LANE - 1).bit_length())
    blk = lambda w: pl.BlockSpec((1, LANE, w), lambda bb, i: (bb, i, 0))
    seq = lambda w: pl.BlockSpec((1, tp, w), lambda bb, i: (bb, 0, 0))
    kern = functools.partial(_attn_kernel, k_top=k_top, pos_bits=pos_bits, n_chunks=n_chunks)
    return pl.pallas_call(
        kern,
        grid=(b, n_chunks),
        in_specs=[blk(512), blk(256), blk(LANE), seq(512), seq(256), seq(LANE)],
        out_specs=blk(512),
        out_shape=jax.ShapeDtypeStruct((b, tp, 512), BF16),
        scratch_shapes=[pltpu.VMEM((n_chunks + 1, LANE, LANE), I32),
                        pltpu.VMEM((N_KV_HEADS, n_chunks, HEAD_DIM + 16, LANE), BF16),
                        pltpu.VMEM((N_KV_HEADS, HEAD_DIM + 16, 4 * LANE), F32),
                        pltpu.VMEM((N_KV_HEADS, 1, 4 * LANE), F32)],
        compiler_params=_cparams("arbitrary", "arbitrary"),
        name="prompt_attend",
    )(q, qi, wi, kpad, kib, v)


def _shifted(cur, prev8):
    cat = jnp.concatenate([prev8, cur], axis=0)
    return pltpu.roll(cat, 1, 0)[SUBLANE:], pltpu.roll(cat, 2, 0)[SUBLANE:]


def _mix_kernel(*refs, tiles_per_seq, tm, has_state, seq_len):
    if has_state:
        (x_ref, a_ref, cb_ref, ccx_ref, prev_ref, sga_ref, sgb_ref, st1_ref, st2_ref,
         wpa_ref, wpb_ref, wo_ref, cw_ref, o_ref) = refs
    else:
        (x_ref, a_ref, cb_ref, ccx_ref, prev_ref, sga_ref, sgb_ref,
         wpa_ref, wpb_ref, wo_ref, cw_ref, o_ref) = refs
    i = pl.program_id(0)
    c = ccx_ref[...]
    m1, m2 = _shifted(c, prev_ref[...])
    r = lax.broadcasted_iota(I32, (tm, 1), 0)
    if has_state:
        pos = r & (seq_len - 1)
        m1 = jnp.where(pos >= 1, m1, st1_ref[...])
        m2 = jnp.where(pos >= 2, m2, st2_ref[...])
    else:
        pos = (i % tiles_per_seq) * tm + r
        m1 = jnp.where(pos >= 1, m1, 0.0)
        m2 = jnp.where(pos >= 2, m2, 0.0)
    cw = cw_ref[...]
    u = m2 * cw[0:1] + m1 * cw[1:2] + c * cw[2:3]
    bmix = (cb_ref[...] * u).astype(BF16)
    pa = jnp.dot(a_ref[...], wpa_ref[...], preferred_element_type=F32)
    pb = jnp.dot(bmix, wpb_ref[...], preferred_element_type=F32)
    mixed = (sga_ref[...] * pa + sgb_ref[...] * pb).astype(BF16)
    o_ref[...] = x_ref[...] + jnp.dot(mixed, wo_ref[...], preferred_element_type=F32)


def _mix(x, a, cb, ccx, sga, sgb, wpa, wpb, wo, cw, tm, tiles_per_seq, state=None, seq_len=1):
    n = x.shape[0]
    row = lambda w: pl.BlockSpec((tm, w), lambda i: (i, 0))
    prev = pl.BlockSpec((SUBLANE, CONV_DIM), lambda i: (jnp.maximum(i * (tm // SUBLANE) - 1, 0), 0))
    in_specs = [row(D_MODEL), row(512), row(CONV_DIM), row(CONV_DIM), prev, row(D_MODEL), row(D_MODEL)]
    args = [x, a, cb, ccx, ccx, sga, sgb]
    if state is not None:
        in_specs += [row(CONV_DIM), row(CONV_DIM)]
        args += list(state)
    in_specs += [_const_spec(wpa.shape), _const_spec(wpb.shape), _const_spec(wo.shape), _const_spec(cw.shape)]
    args += [wpa, wpb, wo, cw]
    kern = functools.partial(_mix_kernel, tiles_per_seq=tiles_per_seq, tm=tm,
                             has_state=state is not None, seq_len=seq_len)
    return pl.pallas_call(
        kern, grid=(n // tm,), in_specs=in_specs, out_specs=row(D_MODEL),
        out_shape=jax.ShapeDtypeStruct((n, D_MODEL), F32),
        compiler_params=_cparams("arbitrary"), name="mix",
    )(*args)


def _ffn_kernel(*refs, tiles_per_seq, tm, has_state, seq_len, n_ff, tail_tile, tail_off, tail_rows):
    if has_state:
        (x_ref, g2_ref, gf_ref, wa_ref, wb_ref, wd_ref, fw_ref, st1_ref, st2_ref,
         y_ref, tail_ref, hn_ref, acc_ref, prev_ref) = refs
    else:
        (x_ref, g2_ref, gf_ref, wa_ref, wb_ref, wd_ref, fw_ref,
         y_ref, tail_ref, hn_ref, acc_ref, prev_ref) = refs
    i = pl.program_id(0)
    x = x_ref[...]
    rinv = lax.rsqrt(jnp.mean(x * x, axis=-1, keepdims=True) + NORM_EPS)
    hn_ref[...] = ((x * rinv) * g2_ref[...]).astype(BF16)
    acc_ref[...] = jnp.zeros(acc_ref.shape, F32)
    r = lax.broadcasted_iota(I32, (tm, 1), 0)
    pos = (r & (seq_len - 1)) if has_state else (i % tiles_per_seq) * tm + r
    write_tail = (i % tiles_per_seq) == tail_tile

    @pl.when((i % tiles_per_seq) == 0)
    def _():
        prev_ref[...] = jnp.zeros(prev_ref.shape, F32)

    def chunk(j, carry):
        hn = hn_ref[...]
        ua = jnp.dot(hn, wa_ref[j], preferred_element_type=F32)
        ub = jnp.dot(hn, wb_ref[j], preferred_element_type=F32)
        m1, m2 = _shifted(ua, prev_ref[j])
        prev_ref[j] = ua[tm - SUBLANE:]
        if has_state:
            m1 = jnp.where(pos >= 1, m1, st1_ref[j])
            m2 = jnp.where(pos >= 2, m2, st2_ref[j])
        else:
            m1 = jnp.where(pos >= 1, m1, 0.0)
            m2 = jnp.where(pos >= 2, m2, 0.0)
        fw = fw_ref[j]
        uc = m2 * fw[0:1] + m1 * fw[1:2] + ua * fw[2:3]
        act = (jax.nn.silu(uc) * ub).astype(BF16)
        acc_ref[...] += jnp.dot(act, wd_ref[j], preferred_element_type=F32)

        @pl.when(write_tail)
        def _():
            tail_ref[0, j] = ua[tail_off:tail_off + tail_rows]
        return carry

    lax.fori_loop(0, n_ff, chunk, 0)
    x3 = x + acc_ref[...]
    r3 = lax.rsqrt(jnp.mean(x3 * x3, axis=-1, keepdims=True) + NORM_EPS)
    y_ref[...] = (x3 * r3) * gf_ref[...]


def _ffn(x, g2, gf, wa, wb, wd, fw, tm, tiles_per_seq, tail_tile, tail_off, tail_rows, state=None, seq_len=1):
    n = x.shape[0]
    n_ff = wa.shape[0]
    n_seq = n // (tm * tiles_per_seq)
    row = lambda w: pl.BlockSpec((tm, w), lambda i: (i, 0))
    in_specs = [row(D_MODEL), _const_spec(g2.shape), _const_spec(gf.shape), _const_spec(wa.shape),
                _const_spec(wb.shape), _const_spec(wd.shape), _const_spec(fw.shape)]
    args = [x, g2, gf, wa, wb, wd, fw]
    if state is not None:
        in_specs += [pl.BlockSpec((n_ff, tm, FF_CHUNK), lambda i: (0, i, 0))] * 2
        args += list(state)
    kern = functools.partial(_ffn_kernel, tiles_per_seq=tiles_per_seq, tm=tm, has_state=state is not None,
                             seq_len=seq_len, n_ff=n_ff, tail_tile=tail_tile, tail_off=tail_off,
                             tail_rows=tail_rows)
    return pl.pallas_call(
        kern, grid=(n // tm,), in_specs=in_specs,
        out_specs=[row(D_MODEL),
                   pl.BlockSpec((1, n_ff, tail_rows, FF_CHUNK), lambda i: (i // tiles_per_seq, 0, 0, 0))],
        out_shape=[jax.ShapeDtypeStruct((n, D_MODEL), F32),
                   jax.ShapeDtypeStruct((n_seq, n_ff, tail_rows, FF_CHUNK), F32)],
        scratch_shapes=[pltpu.VMEM((tm, D_MODEL), BF16), pltpu.VMEM((tm, D_MODEL), F32),
                        pltpu.VMEM((n_ff, SUBLANE, FF_CHUNK), F32)],
        compiler_params=_cparams("arbitrary"), name="ffn",
    )(*args)


def _page_copies(pt_ref, cache_ref, buf, sem, b, slot, n_pages, start):
    def body(p, carry):
        page = pt_ref[b, p] if start else 0
        cp = pltpu.make_async_copy(cache_ref.at[page],
                                   buf.at[slot, pl.ds(pl.multiple_of(p * PAGE_SIZE, PAGE_SIZE), PAGE_SIZE), :],
                                   sem.at[slot])
        if start:
            cp.start()
        else:
            cp.wait()
        return carry
    lax.fori_loop(0, n_pages, body, 0)


def _sample_score_kernel(pt_ref, qi_ref, wi_ref, kin_ref, cache_ref, o_ref, buf, sem, *, n_pages, dec_seq, chunk):
    b = pl.program_id(0)
    nb = pl.num_programs(0)
    slot = b % 2

    @pl.when(b == 0)
    def _():
        _page_copies(pt_ref, cache_ref, buf, sem, b, slot, n_pages, True)

    @pl.when(b + 1 < nb)
    def _():
        _page_copies(pt_ref, cache_ref, buf, sem, b + 1, 1 - slot, n_pages, True)

    _page_copies(pt_ref, cache_ref, buf, sem, b, slot, n_pages, False)

    qi = qi_ref[0]
    wcol = wi_ref[0][:, 0:1]

    def head_sum(s):
        rs = jnp.maximum(s, 0.0) * wcol
        t = rs[0:SUBLANE] + rs[SUBLANE:2 * SUBLANE]
        return t + pltpu.roll(t, dec_seq, 0)

    past = n_pages * PAGE_SIZE
    for c in range(past // chunk):
        kc = buf[slot, c * chunk:(c + 1) * chunk, :].astype(BF16)
        s = lax.dot_general(qi, kc, _NT, preferred_element_type=F32)
        o_ref[0, :, c * chunk:(c + 1) * chunk] = head_sum(s)
    s = head_sum(lax.dot_general(qi, kin_ref[0], _NT, preferred_element_type=F32))
    rr = lax.broadcasted_iota(I32, (SUBLANE, LANE), 0)
    cc = lax.broadcasted_iota(I32, (SUBLANE, LANE), 1)
    o_ref[0, :, past:past + LANE] = jnp.where((cc <= rr) & (cc < dec_seq), s, -jnp.inf)


def _sample_scores(page_table, qi_s, wi_s, kin, cache_kidx, dec_seq):
    nb, n_pages = page_table.shape
    past = n_pages * PAGE_SIZE
    width = past + LANE
    kern = functools.partial(_sample_score_kernel, n_pages=n_pages, dec_seq=dec_seq, chunk=2048)
    grid_spec = pltpu.PrefetchScalarGridSpec(
        num_scalar_prefetch=1, grid=(nb,),
        in_specs=[pl.BlockSpec((1, 16, IDX_DIM), lambda b, pt: (b, 0, 0)),
                  pl.BlockSpec((1, 16, LANE), lambda b, pt: (b, 0, 0)),
                  pl.BlockSpec((1, LANE, IDX_DIM), lambda b, pt: (b, 0, 0)),
                  pl.BlockSpec(memory_space=pl.ANY)],
        out_specs=pl.BlockSpec((1, SUBLANE, width), lambda b, pt: (b, 0, 0)),
        scratch_shapes=[pltpu.VMEM((2, past, IDX_DIM), F32), pltpu.SemaphoreType.DMA((2,))])
    return pl.pallas_call(
        kern, grid_spec=grid_spec,
        out_shape=jax.ShapeDtypeStruct((nb, SUBLANE, width), F32),
        compiler_params=_cparams("arbitrary"), name="sample_scores",
    )(page_table, qi_s, wi_s, kin, cache_kidx)


def _sample_threshold_kernel(s_ref, o_ref, kt_ref, *, k_top, pos_bits, n_chunks):
    def tb(c, carry):
        st = s_ref[c].T
        kt_ref[c] = jnp.where(st == -jnp.inf, INT_MIN, _order_key(st))
        return carry
    lax.fori_loop(0, n_chunks, tb, 0)
    kt_ref[n_chunks] = jnp.full((LANE, LANE), INT_MIN, I32)
    thr, cut = _select_threshold(kt_ref, (n_chunks + 1) // 2, k_top, pos_bits)
    o_ref[...] = jnp.zeros(o_ref.shape, I32)
    o_ref[0:1, :] = thr
    o_ref[1:2, :] = cut


def _sample_threshold(s3, k_top):
    n_chunks, nq, _ = s3.shape
    width = n_chunks * LANE
    kern = functools.partial(_sample_threshold_kernel, k_top=k_top, pos_bits=width.bit_length(), n_chunks=n_chunks)
    return pl.pallas_call(
        kern, grid=(1,),
        in_specs=[_const_spec(s3.shape)],
        out_specs=pl.BlockSpec((SUBLANE, LANE), lambda i: (0, 0)),
        out_shape=jax.ShapeDtypeStruct((SUBLANE, LANE), I32),
        scratch_shapes=[pltpu.VMEM((n_chunks + 2, LANE, LANE), I32)],
        compiler_params=_cparams("arbitrary"), name="sample_threshold",
    )(s3)


def _sample_attn_kernel(pt_ref, q_ref, s_ref, thr_ref, cut_ref, knew_ref, vnew_ref, ck_ref, cv_ref, o_ref,
                        kbuf, vbuf, ksem, vsem, *, n_pages, chunk):
    b = pl.program_id(0)
    nb = pl.num_programs(0)
    slot = b % 2

    @pl.when(b == 0)
    def _():
        _page_copies(pt_ref, ck_ref, kbuf, ksem, b, slot, n_pages, True)
        _page_copies(pt_ref, cv_ref, vbuf, vsem, b, slot, n_pages, True)

    @pl.when(b + 1 < nb)
    def _():
        _page_copies(pt_ref, ck_ref, kbuf, ksem, b + 1, 1 - slot, n_pages, True)
        _page_copies(pt_ref, cv_ref, vbuf, vsem, b + 1, 1 - slot, n_pages, True)

    _page_copies(pt_ref, ck_ref, kbuf, ksem, b, slot, n_pages, False)
    _page_copies(pt_ref, cv_ref, vbuf, vsem, b, slot, n_pages, False)

    qst = q_ref[0]
    thr = thr_ref[0]
    cut = cut_ref[0]
    nh = N_HEADS
    past = n_pages * PAGE_SIZE

    def step(kc, vc, sc, base, m, l, acc):
        width = sc.shape[1]
        key = jnp.where(sc == -jnp.inf, INT_MIN, _order_key(sc))
        pos = base + lax.broadcasted_iota(I32, (SUBLANE, width), 1)
        sel = (key > thr) | ((key == thr) & (pos <= cut))
        bias = jnp.where(sel, 0.0, NEG)
        lg = lax.dot_general(qst, kc, _NT, preferred_element_type=F32)
        lg = (lg.reshape(nh, SUBLANE, width) + bias[None]).reshape(nh * SUBLANE, width)
        m_new = jnp.maximum(m, jnp.max(lg, axis=-1, keepdims=True))
        alpha = jnp.exp(m - m_new)
        p = jnp.exp(lg - m_new)
        l = alpha * l + jnp.sum(p, axis=-1, keepdims=True)
        acc = alpha * acc + jnp.dot(p.astype(BF16), vc, preferred_element_type=F32)
        return m_new, l, acc

    m = jnp.full((nh * SUBLANE, 1), NEG, F32)
    l = jnp.zeros((nh * SUBLANE, 1), F32)
    acc = jnp.zeros((nh * SUBLANE, LANE), F32)
    for c in range(past // chunk):
        kc = kbuf[slot, c * chunk:(c + 1) * chunk, :].astype(BF16)
        vc = vbuf[slot, c * chunk:(c + 1) * chunk, :].astype(BF16)
        m, l, acc = step(kc, vc, s_ref[0, :, c * chunk:(c + 1) * chunk], c * chunk, m, l, acc)
    m, l, acc = step(knew_ref[0], vnew_ref[0], s_ref[0, :, past:past + LANE], past, m, l, acc)
    o_ref[0] = acc / l


def _sample_attend(page_table, qst, s_all, thr, cut, knew, vnew, cache_k, cache_v):
    nb, n_pages = page_table.shape
    past = n_pages * PAGE_SIZE
    width = past + LANE
    kern = functools.partial(_sample_attn_kernel, n_pages=n_pages, chunk=2048)
    per_b = lambda shape: pl.BlockSpec((1,) + shape, lambda b, pt: (b,) + (0,) * len(shape))
    grid_spec = pltpu.PrefetchScalarGridSpec(
        num_scalar_prefetch=1, grid=(nb,),
        in_specs=[per_b((N_HEADS * SUBLANE, LANE)), per_b((SUBLANE, width)), per_b((SUBLANE, 1)),
                  per_b((SUBLANE, 1)), per_b((LANE, LANE)), per_b((LANE, LANE)),
                  pl.BlockSpec(memory_space=pl.ANY), pl.BlockSpec(memory_space=pl.ANY)],
        out_specs=per_b((N_HEADS * SUBLANE, LANE)),
        scratch_shapes=[pltpu.VMEM((2, past, LANE), F32), pltpu.VMEM((2, past, LANE), F32),
                        pltpu.SemaphoreType.DMA((2,)), pltpu.SemaphoreType.DMA((2,))])
    return pl.pallas_call(
        kern, grid_spec=grid_spec,
        out_shape=jax.ShapeDtypeStruct((nb, N_HEADS * SUBLANE, LANE), F32),
        compiler_params=_cparams("arbitrary"), name="sample_attend",
    )(page_table, qst, s_all, thr, cut, knew, vnew, cache_k, cache_v)


def _rope_tables(pos):
    rot = HEAD_DIM // ROPE_FRACTION
    half = rot // 2
    inv = jnp.power(ROPE_THETA, -jnp.arange(half, dtype=F32) * 2.0 / rot)
    ang = pos.astype(F32)[:, None] * inv[None, :]
    cos, sin = jnp.cos(ang), jnp.sin(ang)
    n = pos.shape[0]
    one = jnp.ones((n, HEAD_DIM - rot), F32)
    zero = jnp.zeros((n, HEAD_DIM - rot), F32)
    zh = jnp.zeros((n, half), F32)
    cos_t = jnp.concatenate([cos, cos, one] * 2, axis=1)
    sin_a = jnp.concatenate([zh, sin, zero] * 2, axis=1)
    sin_b = jnp.concatenate([-sin, zh, zero] * 2, axis=1)
    return cos_t, sin_a, sin_b


def _split_w_in(w):
    sizes = [N_HEADS * HEAD_DIM, N_KV_HEADS * HEAD_DIM, N_KV_HEADS * HEAD_DIM, N_IDX_HEADS * IDX_DIM, IDX_DIM,
             N_IDX_HEADS, CONV_DIM, CONV_DIM, CONV_DIM, D_MODEL, D_MODEL]
    parts, o = [], 0
    for s in sizes:
        parts.append(w[:, o:o + s])
        o += s
    q, k, v, qi, ki, wi, cb, cc, cx, ga, gb = parts
    z = lambda c: jnp.zeros((w.shape[0], c), w.dtype)
    watt = jnp.concatenate([q, k, v, qi, ki, z(LANE - IDX_DIM), wi, z(LANE - N_IDX_HEADS)], axis=1).astype(BF16)
    wc = jnp.concatenate([cb, cc, cx], axis=1).astype(BF16)
    wg = jnp.concatenate([ga, gb], axis=1).astype(BF16)
    return watt, wc, wg


def _conv_state_rows(state, seq):
    b, _, c = state.shape
    z = jnp.zeros((b, seq, c), state.dtype)
    st1 = z.at[:, 0].set(state[:, 1])
    st2 = z.at[:, 0].set(state[:, 0]).at[:, 1].set(state[:, 1])
    return st1.reshape(b * seq, c), st2.reshape(b * seq, c)


def kernel(x_prompt, x_sample, cache_k, cache_v, cache_kidx, state_conv, state_ffn_conv, page_table, meta_tokens,
           w_in, w_pa, w_pb, w_o, conv_w, norm1, norm2, ffn_up, ffn_conv_w, ffn_down, norm_final):
    depth = w_in.shape[0]
    assert depth == 1, "single-layer step"
    bp, seq, _ = x_prompt.shape
    bs, dec_seq, _ = x_sample.shape
    n_pages = page_table.shape[1]
    past = n_pages * PAGE_SIZE
    t_p = seq + N_META
    tp = -(-t_p // ROW_TILE) * ROW_TILE
    assert tp % LANE == 0 and ROW_TILE % SUBLANE == 0
    assert N_IDX_HEADS * dec_seq == 2 * SUBLANE and 2 * dec_seq == SUBLANE, "sample row packing"
    k_top_p = min(TOPK_MAX, t_p // TOPK_FRACTION)
    k_top_s = min(TOPK_MAX, (past + dec_seq) // TOPK_FRACTION)
    n_ff = D_FF // FF_CHUNK
    ns = bs * dec_seq

    watt, wc, wg = _split_w_in(w_in[0])
    g1 = norm1[0][None, :]
    g2 = norm2[0][None, :]
    gf = norm_final[None, :]
    wpa = w_pa[0].astype(BF16)
    wpb = w_pb[0].astype(BF16)
    wo = w_o[0].astype(BF16)
    cw = jnp.zeros((SUBLANE, CONV_DIM), F32).at[:CONV_WIDTH].set(conv_w[0])
    up = ffn_up[0].astype(BF16)
    wa = up[:, :D_FF].reshape(D_MODEL, n_ff, FF_CHUNK).transpose(1, 0, 2)
    wb = up[:, D_FF:].reshape(D_MODEL, n_ff, FF_CHUNK).transpose(1, 0, 2)
    wd = ffn_down[0].astype(BF16).reshape(n_ff, FF_CHUNK, D_MODEL)
    fw = jnp.zeros((SUBLANE, D_FF), F32).at[:CONV_WIDTH].set(ffn_conv_w[0])
    fw = fw.reshape(SUBLANE, n_ff, FF_CHUNK).transpose(1, 0, 2)

    meta = jnp.broadcast_to(meta_tokens[None].astype(x_prompt.dtype), (bp, N_META, D_MODEL))
    h_p = jnp.concatenate([meta, x_prompt, jnp.zeros((bp, tp - t_p, D_MODEL), x_prompt.dtype)], axis=1)
    h_p = h_p.reshape(bp * tp, D_MODEL)
    tabs_p = _rope_tables(jnp.arange(tp))
    (q, k, v, kpad, qi, ki, kib, wi, cb, ccx, sga, sgb) = _proj(h_p, g1, watt, wc, wg, *tabs_p, ROW_TILE)
    r3 = lambda a: a.reshape(bp, tp, a.shape[-1])
    a_p = _prompt_attend(r3(q), r3(qi), r3(wi), r3(kpad), r3(kib), r3(v), k_top_p).reshape(bp * tp, 512)
    tiles = tp // ROW_TILE
    x2 = _mix(h_p, a_p, cb, ccx, sga, sgb, wpa, wpb, wo, cw, ROW_TILE, tiles)
    tail_row = t_p - SUBLANE
    y_p, tail_p = _ffn(x2, g2, gf, wa, wb, wd, fw, ROW_TILE, tiles,
                       tail_tile=tail_row // ROW_TILE, tail_off=tail_row % ROW_TILE, tail_rows=SUBLANE)
    y_prompt = y_p.reshape(bp, tp, D_MODEL)[:, N_META:t_p]
    k_prompt = r3(k)[:, :t_p].reshape(1, bp, t_p, N_KV_HEADS, HEAD_DIM)
    v_prompt = r3(v)[:, :t_p].reshape(1, bp, t_p, N_KV_HEADS, HEAD_DIM)
    kidx_prompt = r3(ki)[:, :t_p][None]
    conv_prompt = r3(ccx)[:, t_p - (CONV_WIDTH - 1):t_p][None]
    ffn_prompt = tail_p.transpose(0, 2, 1, 3).reshape(bp, SUBLANE, D_FF)[:, SUBLANE - (CONV_WIDTH - 1):][None]

    h_s = x_sample.reshape(ns, D_MODEL)
    pos_s = jnp.tile(past + jnp.arange(dec_seq), bs)
    (q_s, k_s, v_s, _, qi_s, ki_s, _, wi_s, cb_s, ccx_s, sga_s, sgb_s) = _proj(
        h_s, g1, watt, wc, wg, *_rope_tables(pos_s), ns)
    qi_hq = qi_s.reshape(bs, dec_seq, N_IDX_HEADS, IDX_DIM).transpose(0, 2, 1, 3).reshape(bs, 16, IDX_DIM)
    wi_hq = wi_s[:, :N_IDX_HEADS].reshape(bs, dec_seq, N_IDX_HEADS).transpose(0, 2, 1).reshape(bs, 16, 1)
    wi_hq = jnp.broadcast_to(wi_hq, (bs, 16, LANE))
    pad_rows = lambda a: jnp.zeros((bs, LANE, a.shape[-1]), a.dtype).at[:, :dec_seq].set(
        a.reshape(bs, dec_seq, a.shape[-1]))
    kin = pad_rows(ki_s.astype(BF16))
    s_all = _sample_scores(page_table, qi_hq, wi_hq, kin, cache_kidx[0], dec_seq)
    s3 = s_all[:, :dec_seq].reshape(ns, (past + LANE) // LANE, LANE).transpose(1, 0, 2)
    sel = _sample_threshold(s3, k_top_s)
    col8 = lambda v_, fill: jnp.full((bs, SUBLANE, 1), fill, I32).at[:, :dec_seq, 0].set(v_.reshape(bs, dec_seq))
    thr_s = col8(sel[0], jnp.iinfo(jnp.int32).max)
    cut_s = col8(sel[1], -1)
    qh = q_s.reshape(bs, dec_seq, N_HEADS, HEAD_DIM).transpose(0, 2, 1, 3)
    qst = jnp.zeros((bs, N_HEADS, SUBLANE, N_KV_HEADS, HEAD_DIM), BF16)
    for h in range(N_HEADS):
        qst = qst.at[:, h, :dec_seq, h // (N_HEADS // N_KV_HEADS)].set(qh[:, h])
    qst = qst.reshape(bs, N_HEADS * SUBLANE, LANE)
    knew = pad_rows(k_s.astype(BF16))
    vnew = pad_rows(v_s.astype(BF16))
    ck = cache_k[0].reshape(-1, PAGE_SIZE, LANE)
    cv = cache_v[0].reshape(-1, PAGE_SIZE, LANE)
    o_s = _sample_attend(page_table, qst, s_all, thr_s, cut_s, knew, vnew, ck, cv)
    o_s = o_s.reshape(bs, N_HEADS, SUBLANE, N_KV_HEADS, HEAD_DIM)[:, :, :dec_seq]
    a_s = jnp.stack([o_s[:, h, :, h // (N_HEADS // N_KV_HEADS)] for h in range(N_HEADS)], axis=2)
    a_s = a_s.reshape(ns, N_HEADS * HEAD_DIM).astype(BF16)
    x2_s = _mix(h_s, a_s, cb_s, ccx_s, sga_s, sgb_s, wpa, wpb, wo, cw, ns, 1,
                state=_conv_state_rows(state_conv[0], dec_seq), seq_len=dec_seq)
    st1f, st2f = _conv_state_rows(state_ffn_conv[0], dec_seq)
    chunked = lambda a: a.reshape(ns, n_ff, FF_CHUNK).transpose(1, 0, 2)
    y_s, tail_s = _ffn(x2_s, g2, gf, wa, wb, wd, fw, ns, 1, tail_tile=0, tail_off=0, tail_rows=ns,
                       state=(chunked(st1f), chunked(st2f)), seq_len=dec_seq)
    y_sample = y_s.reshape(bs, dec_seq, D_MODEL)
    ua_s = tail_s.transpose(0, 2, 1, 3).reshape(bs, dec_seq, D_FF)
    lastc = lambda a, c: a.reshape(bs, dec_seq, c)[:, dec_seq - (CONV_WIDTH - 1):][None]
    return (y_prompt, y_sample, k_prompt, v_prompt, kidx_prompt, conv_prompt, ffn_prompt,
            k_s.reshape(1, bs, dec_seq, N_KV_HEADS, HEAD_DIM), v_s.reshape(1, bs, dec_seq, N_KV_HEADS, HEAD_DIM),
            ki_s.reshape(1, bs, dec_seq, IDX_DIM), lastc(ccx_s, CONV_DIM), lastc(ua_s, D_FF))
```
